```python
import math
import jax, jax.numpy as jnp
from jax import lax
import numpy as np

D_MODEL = 1024
BATCH = 4
SEQ = 4096
DEPTH = 4

D_CONV = D_MODEL
CONV_GROUPS = 8
SHORT_CONV_K = 3
D_RNN = D_MODEL
RNN_HEADS = 4
RNN_HEAD_DIM = D_RNN // RNN_HEADS
RNN_CONV_K = 4
LRU_C = 8.0
N_BRANCH = 2
D_FF = 2816
D_IN = 3 * D_CONV + 2 * D_RNN + N_BRANCH * D_MODEL
SPLITS = (D_CONV, 2 * D_CONV, 3 * D_CONV, 3 * D_CONV + D_RNN, 3 * D_CONV + 2 * D_RNN)
ALPHA = (2.0 * DEPTH) ** 0.25
BETA = (8.0 * DEPTH) ** -0.25
LN_EPS = 1e-5

kernel_name = "hybrid_shortconv_rglru_macaron_deepnorm"


def layer_norm(x, g, b):
    xf = x.astype(jnp.float32)
    mu = jnp.mean(xf, axis=-1, keepdims=True)
    var = jnp.mean(jnp.square(xf - mu), axis=-1, keepdims=True)
    y = (xf - mu) * lax.rsqrt(var + LN_EPS)
    return (y * g.astype(jnp.float32) + b.astype(jnp.float32)).astype(x.dtype)


def causal_depthwise_conv(x, w, k):
    s = x.shape[1]
    xp = jnp.pad(x, ((0, 0), (k - 1, 0), (0, 0)))
    out = xp[:, 0:s, :] * w[0]
    for i in range(1, k):
        out = out + xp[:, i:i + s, :] * w[i]
    return out


def swiglu(x, w1, w2):
    g, u = jnp.split(x @ w1, 2, axis=-1)
    return (jax.nn.silu(g) * u) @ w2


def rg_lru(x, gate_in, gate_rec, a_param):
    s = x.shape[1]
    log_a = -LRU_C * gate_rec * jax.nn.softplus(-a_param)
    a = jnp.exp(log_a)
    mult = jnp.sqrt(-jnp.expm1(2.0 * log_a))
    is_start = (jnp.arange(s) == 0)[None, :, None]
    mult = jnp.where(is_start, 1.0, mult)
    b = x * gate_in * mult

    def combine(left, right):
        a1, b1 = left
        a2, b2 = right
        return a1 * a2, a2 * b1 + b2

    _, h = lax.associative_scan(combine, (a, b), axis=1)
    return h


def hybrid_mixer(x, w_in, b_merge, sc_w, rc_w, rc_b, rg_w, rg_b, a_param,
                 w_out_conv, w_out_rnn, w_o):
    bsz, s, _ = x.shape
    proj = x @ w_in
    b_gate, c_gate, v, x_r, y_r, g_logits = jnp.split(proj, SPLITS, axis=-1)

    y_a = (b_gate * causal_depthwise_conv(c_gate * v, sc_w, SHORT_CONV_K)) @ w_out_conv

    xc = causal_depthwise_conv(x_r, rc_w, RNN_CONV_K) + rc_b
    xh = xc.reshape(bsz, s, RNN_HEADS, RNN_HEAD_DIM)
    gates = jnp.einsum('bshi,ghij->gbshj', xh, rg_w).reshape(2, bsz, s, D_RNN)
    gates = jax.nn.sigmoid((gates + rg_b[:, None, None, :]).astype(jnp.float32))
    h = rg_lru(xc.astype(jnp.float32), gates[0], gates[1],
               a_param.astype(jnp.float32)).astype(x.dtype)
    y_b = (h * jax.nn.gelu(y_r)) @ w_out_rnn

    g_a, g_b = jnp.split(jax.nn.sigmoid(g_logits + b_merge), 2, axis=-1)
    return (g_a * y_a + g_b * y_b) @ w_o


def setup_inputs(seed: int = 0) -> dict:
    key = jax.random.key(seed)
    ks = jax.random.split(key, 20)
    f32 = jnp.float32
    nrm = lambda k, shape, scale: jax.random.normal(k, shape, f32) * scale

    x = jax.random.normal(ks[0], (BATCH, SEQ, D_MODEL), f32)
    w_in = nrm(ks[1], (DEPTH, D_MODEL, D_IN), D_MODEL ** -0.5)
    b_merge = nrm(ks[2], (DEPTH, N_BRANCH * D_MODEL), 0.1)
    sc_w = nrm(ks[3], (DEPTH, SHORT_CONV_K, D_CONV), SHORT_CONV_K ** -0.5)
    rc_w = nrm(ks[4], (DEPTH, RNN_CONV_K, D_RNN), RNN_CONV_K ** -0.5)
    rc_b = nrm(ks[5], (DEPTH, D_RNN), 0.02)
    rg_w = nrm(ks[6], (DEPTH, 2, RNN_HEADS, RNN_HEAD_DIM, RNN_HEAD_DIM), RNN_HEAD_DIM ** -0.5)
    rg_b = nrm(ks[7], (DEPTH, 2, D_RNN), 0.1)
    u = jax.random.uniform(ks[8], (DEPTH, D_RNN), f32, 0.9, 0.999)
    s_a = u ** (1.0 / LRU_C)
    a_param = jnp.log(s_a) - jnp.log1p(-s_a)
    w_out_conv = nrm(ks[9], (DEPTH, D_CONV, D_MODEL), D_CONV ** -0.5)
    w_out_rnn = nrm(ks[10], (DEPTH, D_RNN, D_MODEL), D_RNN ** -0.5)
    w_o = nrm(ks[11], (DEPTH, D_MODEL, D_MODEL), BETA * D_MODEL ** -0.5)
    ffn_w1 = nrm(ks[12], (DEPTH, 2, D_MODEL, 2 * D_FF), D_MODEL ** -0.5)
    ffn_w2 = nrm(ks[13], (DEPTH, 2, D_FF, D_MODEL), BETA * D_FF ** -0.5)
    ln_g = 1.0 + nrm(ks[14], (DEPTH, 3, D_MODEL), 0.02)
    ln_b = nrm(ks[15], (DEPTH, 3, D_MODEL), 0.02)
    return {"x": x, "w_in": w_in, "b_merge": b_merge, "sc_w": sc_w, "rc_w": rc_w,
            "rc_b": rc_b, "rg_w": rg_w, "rg_b": rg_b, "a_param": a_param,
            "w_out_conv": w_out_conv, "w_out_rnn": w_out_rnn, "w_o": w_o,
            "ffn_w1": ffn_w1, "ffn_w2": ffn_w2, "ln_g": ln_g, "ln_b": ln_b}


def reference(x, w_in, b_merge, sc_w, rc_w, rc_b, rg_w, rg_b, a_param,
              w_out_conv, w_out_rnn, w_o, ffn_w1, ffn_w2, ln_g, ln_b):
    for l in range(DEPTH):
        x = layer_norm(ALPHA * x + 0.5 * swiglu(x, ffn_w1[l, 0], ffn_w2[l, 0]),
                       ln_g[l, 0], ln_b[l, 0])
        mix = hybrid_mixer(x, w_in[l], b_merge[l], sc_w[l], rc_w[l], rc_b[l], rg_w[l],
                           rg_b[l], a_param[l], w_out_conv[l], w_out_rnn[l], w_o[l])
        x = layer_norm(ALPHA * x + mix, ln_g[l, 1], ln_b[l, 1])
        x = layer_norm(ALPHA * x + 0.5 * swiglu(x, ffn_w1[l, 1], ffn_w2[l, 1]),
                       ln_g[l, 2], ln_b[l, 2])
    return x
```

```python
import functools
import math

import jax
import jax.numpy as jnp
from jax import lax
from jax.experimental import pallas as pl
from jax.experimental.pallas import tpu as pltpu

LRU_C = 8.0
LN_EPS = 1e-5
SHORT_CONV_K = 3
RNN_CONV_K = 4

V7X_SUBLANES = 8
V7X_MXU_COLS = 256
V7X_VMEM_BYTES = 64 * 1024 * 1024
VMEM_LIMIT_BYTES = V7X_VMEM_BYTES - 8 * 1024 * 1024

F32 = jnp.float32
BF16 = jnp.bfloat16


def _sigmoid(x):
    return 0.5 * jnp.tanh(0.5 * x) + 0.5


def _gelu_tanh(x):
    c = math.sqrt(2.0 / math.pi)
    return 0.5 * x * (1.0 + jnp.tanh(c * (x + 0.044715 * (x * x * x))))


def _layer_norm(z, g, b):
    mu = jnp.mean(z, axis=-1, keepdims=True)
    zc = z - mu
    var = jnp.mean(zc * zc, axis=-1, keepdims=True)
    return zc * lax.rsqrt(var + LN_EPS) * g + b


def _dot(a, b):
    return jnp.dot(a, b, preferred_element_type=F32)


def _ffn_kernel(x_ref, w1_ref, w2_ref, g_ref, b_ref, o_ref, act_ref, *, d_ff, fc, alpha):
    x = x_ref[...]
    xb = x.astype(BF16)
    for c in range(d_ff // fc):
        gate = _dot(xb, w1_ref[:, c * fc:(c + 1) * fc])
        up = _dot(xb, w1_ref[:, d_ff + c * fc:d_ff + (c + 1) * fc])
        act_ref[:, c * fc:(c + 1) * fc] = (gate * _sigmoid(gate) * up).astype(BF16)
    y = _dot(act_ref[...], w2_ref[...])
    o_ref[...] = _layer_norm(alpha * x + 0.5 * y, g_ref[...], b_ref[...])


def _ffn_ln(x2d, w1, w2, ln_g, ln_b, layer, which, ln_idx, *, alpha, tm):
    n, d = x2d.shape
    d_ff = w2.shape[2]
    fc = V7X_MXU_COLS
    assert n % tm == 0 and d_ff % fc == 0
    const = dict(pipeline_mode=pl.Buffered(1))
    return pl.pallas_call(
        functools.partial(_ffn_kernel, d_ff=d_ff, fc=fc, alpha=alpha),
        grid=(n // tm,),
        in_specs=[
            pl.BlockSpec((tm, d), lambda i: (i, 0)),
            pl.BlockSpec((None, None, d, 2 * d_ff), lambda i: (layer, which, 0, 0), **const),
            pl.BlockSpec((None, None, d_ff, d), lambda i: (layer, which, 0, 0), **const),
            pl.BlockSpec((None, None, 1, d), lambda i: (layer, ln_idx, 0, 0), **const),
            pl.BlockSpec((None, None, 1, d), lambda i: (layer, ln_idx, 0, 0), **const),
        ],
        out_specs=pl.BlockSpec((tm, d), lambda i: (i, 0)),
        out_shape=jax.ShapeDtypeStruct((n, d), F32),
        scratch_shapes=[pltpu.VMEM((tm, d_ff), BF16)],
        compiler_params=pltpu.CompilerParams(
            dimension_semantics=("arbitrary",), vmem_limit_bytes=VMEM_LIMIT_BYTES),
        name=f"ffn_ln_{layer}_{which}",
    )(x2d, w1, w2, ln_g, ln_b)


def _causal_conv(buf_ref, w_ref, k, ts):
    halo = V7X_SUBLANES
    out = buf_ref[halo:halo + ts, :] * w_ref[k - 1:k, :]
    for i in range(1, k):
        out = out + buf_ref[halo - i:halo - i + ts, :] * w_ref[k - 1 - i:k - i, :]
    return out


def _mixer_kernel(x_ref, win_ref, bm_ref, scw_ref, rcw_ref, rcb_ref, rgw_ref, rgb_ref, ap_ref,
                  woc_ref, wor_ref, wo_ref, g_ref, b_ref, o_ref,
                  cv_ref, xr_ref, a_ref, bb_ref, h_ref, carry_ref,
                  *, ts, dc, dr, dm, heads, alpha):
    halo = V7X_SUBLANES
    s = pl.program_id(1)

    @pl.when(s == 0)
    def _():
        cv_ref[0:halo, :] = jnp.zeros((halo, dc), F32)
        xr_ref[0:halo, :] = jnp.zeros((halo, dr), F32)
        carry_ref[...] = jnp.zeros_like(carry_ref)

    x = x_ref[...]
    xb = x.astype(BF16)

    def proj(lo, width):
        return _dot(xb, win_ref[:, lo:lo + width])

    o_b, o_c, o_v = 0, dc, 2 * dc
    o_xr, o_yr, o_g = 3 * dc, 3 * dc + dr, 3 * dc + 2 * dr

    cv_ref[halo:halo + ts, :] = proj(o_c, dc) * proj(o_v, dc)
    conv_a = _causal_conv(cv_ref, scw_ref, SHORT_CONV_K, ts)
    cv_ref[0:halo, :] = cv_ref[ts:ts + halo, :]
    y_a = _dot((proj(o_b, dc) * conv_a).astype(BF16), woc_ref[...])
    mix = _sigmoid(proj(o_g, dm) + bm_ref[:, 0:dm]) * y_a

    xr_ref[halo:halo + ts, :] = proj(o_xr, dr)
    xc = _causal_conv(xr_ref, rcw_ref, RNN_CONV_K, ts) + rcb_ref[...]
    xr_ref[0:halo, :] = xr_ref[ts:ts + halo, :]
    xcb = xc.astype(BF16)
    hd = dr // heads
    gi = jnp.concatenate(
        [_dot(xcb[:, h * hd:(h + 1) * hd], rgw_ref[0, h]) for h in range(heads)], axis=-1)
    gr = jnp.concatenate(
        [_dot(xcb[:, h * hd:(h + 1) * hd], rgw_ref[1, h]) for h in range(heads)], axis=-1)
    gate_in = _sigmoid(gi + rgb_ref[0:1, :])
    gate_rec = _sigmoid(gr + rgb_ref[1:2, :])
    nap = -ap_ref[...]
    softplus = jnp.maximum(nap, 0.0) + jnp.log1p(jnp.exp(-jnp.abs(nap)))
    log_a = (-LRU_C) * gate_rec * softplus
    a = jnp.exp(log_a)
    mult = jnp.sqrt(-jnp.tanh(log_a) * (a * a + 1.0))
    row = lax.broadcasted_iota(jnp.int32, (ts, 1), 0)
    mult = jnp.where(jnp.logical_and(row == 0, s == 0), 1.0, mult)
    bb = xc * gate_in * mult

    ng = ts // halo
    a3 = a.reshape(ng, halo, dr)
    b3 = bb.reshape(ng, halo, dr)
    r3 = lax.broadcasted_iota(jnp.int32, (ng, halo, dr), 1)
    d = 1
    while d < halo:
        keep = r3 >= d
        a_sh = jnp.where(keep, pltpu.roll(a3, d, axis=1), 1.0)
        b_sh = jnp.where(keep, pltpu.roll(b3, d, axis=1), 0.0)
        b3 = a3 * b_sh + b3
        a3 = a3 * a_sh
        d *= 2
    a_ref[...] = a3.reshape(ts, dr)
    bb_ref[...] = b3.reshape(ts, dr)

    def group(gidx, hprev):
        r0 = pl.multiple_of(gidx * halo, halo)
        hg = a_ref[pl.ds(r0, halo), :] * hprev + bb_ref[pl.ds(r0, halo), :]
        h_ref[pl.ds(r0, halo), :] = hg
        return hg[halo - 1:halo, :]

    carry_ref[...] = lax.fori_loop(0, ng, group, carry_ref[...], unroll=8)

    y_b = _dot((h_ref[...] * _gelu_tanh(proj(o_yr, dr))).astype(BF16), wor_ref[...])
    mix = mix + _sigmoid(proj(o_g + dm, dm) + bm_ref[:, dm:2 * dm]) * y_b
    out = _dot(mix.astype(BF16), wo_ref[...])
    o_ref[...] = _layer_norm(alpha * x + out, g_ref[...], b_ref[...])


def _mixer_ln(x3d, w_in, b_merge, sc_w, rc_w, rc_b, rg_w, rg_b, a_param,
              w_out_conv, w_out_rnn, w_o, ln_g, ln_b, layer, *, alpha, ts):
    bsz, seq, dm = x3d.shape
    dc = w_out_conv.shape[1]
    dr = w_out_rnn.shape[1]
    heads = rg_w.shape[2]
    hd = rg_w.shape[3]
    d_in = w_in.shape[2]
    assert seq % ts == 0 and ts % V7X_SUBLANES == 0
    const = dict(pipeline_mode=pl.Buffered(1))
    lsel3 = lambda b, s: (layer, 0, 0)
    halo = V7X_SUBLANES
    kern = functools.partial(_mixer_kernel, ts=ts, dc=dc, dr=dr, dm=dm, heads=heads, alpha=alpha)
    return pl.pallas_call(
        kern,
        grid=(bsz, seq // ts),
        in_specs=[
            pl.BlockSpec((None, ts, dm), lambda b, s: (b, s, 0)),
            pl.BlockSpec((None, dm, d_in), lsel3, **const),
            pl.BlockSpec((None, 1, 2 * dm), lsel3, **const),
            pl.BlockSpec((None, SHORT_CONV_K, dc), lsel3, **const),
            pl.BlockSpec((None, RNN_CONV_K, dr), lsel3, **const),
            pl.BlockSpec((None, 1, dr), lsel3, **const),
            pl.BlockSpec((None, 2, heads, hd, hd), lambda b, s: (layer, 0, 0, 0, 0), **const),
            pl.BlockSpec((None, 2, dr), lsel3, **const),
            pl.BlockSpec((None, 1, dr), lsel3, **const),
            pl.BlockSpec((None, dc, dm), lsel3, **const),
            pl.BlockSpec((None, dr, dm), lsel3, **const),
            pl.BlockSpec((None, dm, dm), lsel3, **const),
            pl.BlockSpec((None, None, 1, dm), lambda b, s: (layer, 1, 0, 0), **const),
            pl.BlockSpec((None, None, 1, dm), lambda b, s: (layer, 1, 0, 0), **const),
        ],
        out_specs=pl.BlockSpec((None, ts, dm), lambda b, s: (b, s, 0)),
        out_shape=jax.ShapeDtypeStruct((bsz, seq, dm), F32),
        scratch_shapes=[
            pltpu.VMEM((ts + halo, dc), F32),
            pltpu.VMEM((ts + halo, dr), F32),
            pltpu.VMEM((ts, dr), F32),
            pltpu.VMEM((ts, dr), F32),
            pltpu.VMEM((ts, dr), F32),
            pltpu.VMEM((1, dr), F32),
        ],
        compiler_params=pltpu.CompilerParams(
            dimension_semantics=("arbitrary", "arbitrary"), vmem_limit_bytes=VMEM_LIMIT_BYTES),
        name=f"mixer_ln_{layer}",
    )(x3d, w_in, b_merge, sc_w, rc_w, rc_b, rg_w, rg_b, a_param,
      w_out_conv, w_out_rnn, w_o, ln_g, ln_b)


def _pick_tile(n, target):
    t = min(n, target)
    while n % t:
        t -= V7X_SUBLANES
    return t


def kernel(x, w_in, b_merge, sc_w, rc_w, rc_b, rg_w, rg_b, a_param, w_out_conv, w_out_rnn, w_o,
           ffn_w1, ffn_w2, ln_g, ln_b):
    bsz, seq, dm = x.shape
    depth = w_in.shape[0]
    alpha = (2.0 * depth) ** 0.25
    tm = _pick_tile(bsz * seq, 512)
    ts = _pick_tile(seq, 512)

    w_in_b = w_in.astype(BF16)
    rg_w_b = rg_w.astype(BF16)
    woc_b = w_out_conv.astype(BF16)
    wor_b = w_out_rnn.astype(BF16)
    wo_b = w_o.astype(BF16)
    w1_b = ffn_w1.astype(BF16)
    w2_b = ffn_w2.astype(BF16)
    bm3 = b_merge[:, None, :]
    rcb3 = rc_b[:, None, :]
    ap3 = a_param[:, None, :]
    g4 = ln_g[:, :, None, :]
    b4 = ln_b[:, :, None, :]

    for l in range(depth):
        x = _ffn_ln(x.reshape(bsz * seq, dm), w1_b, w2_b, g4, b4, l, 0, 0, alpha=alpha, tm=tm)
        x = _mixer_ln(x.reshape(bsz, seq, dm), w_in_b, bm3, sc_w, rc_w, rcb3, rg_w_b, rg_b, ap3,
                      woc_b, wor_b, wo_b, g4, b4, l, alpha=alpha, ts=ts)
        x = _ffn_ln(x.reshape(bsz * seq, dm), w1_b, w2_b, g4, b4, l, 1, 2, alpha=alpha, tm=tm)
    return x.reshape(bsz, seq, dm)
```

```python
import functools
import math

import jax
import jax.numpy as jnp
from jax import lax
from jax.experimental import pallas as pl
from jax.experimental.pallas import tpu as pltpu

LRU_C = 8.0
LN_EPS = 1e-5
SHORT_CONV_K = 3
RNN_CONV_K = 4
OUT_SPLIT = 2

V7X_SUBLANES = 8
V7X_MXU_COLS = 256
V7X_VMEM_BYTES = 64 * 1024 * 1024
VMEM_LIMIT_BYTES = V7X_VMEM_BYTES - 8 * 1024 * 1024

F32 = jnp.float32
BF16 = jnp.bfloat16


def _sigmoid(x):
    return 0.5 * jnp.tanh(0.5 * x) + 0.5


def _gelu_tanh(x):
    c = math.sqrt(2.0 / math.pi)
    return 0.5 * x * (1.0 + jnp.tanh(c * (x + 0.044715 * (x * x * x))))


def _layer_norm(z, g, b):
    mu = jnp.mean(z, axis=-1, keepdims=True)
    zc = z - mu
    var = jnp.mean(zc * zc, axis=-1, keepdims=True)
    return zc * lax.rsqrt(var + LN_EPS) * g + b


_dot = functools.partial(jnp.dot, preferred_element_type=F32)


def _ffn_kernel(x_ref, w1_ref, w2_ref, g_ref, b_ref, o_ref, act_ref, *, d_ff, fc, alpha):
    x = x_ref[...]
    xb = x.astype(BF16)
    for c in range(d_ff // fc):
        gate = _dot(xb, w1_ref[:, c * fc:(c + 1) * fc])
        up = _dot(xb, w1_ref[:, d_ff + c * fc:d_ff + (c + 1) * fc])
        act_ref[:, c * fc:(c + 1) * fc] = (gate * _sigmoid(gate) * up).astype(BF16)
    y = _dot(act_ref[...], w2_ref[...])
    o_ref[...] = _layer_norm(alpha * x + 0.5 * y, g_ref[...], b_ref[...])


def _ffn_ln(x2d, w1, w2, ln_g, ln_b, layer, which, ln_idx, *, alpha, tm):
    n, d = x2d.shape
    d_ff = w2.shape[2]
    fc = V7X_MXU_COLS
    assert n % tm == 0 and d_ff % fc == 0
    const = dict(pipeline_mode=pl.Buffered(1))
    return pl.pallas_call(
        functools.partial(_ffn_kernel, d_ff=d_ff, fc=fc, alpha=alpha),
        grid=(n // tm,),
        in_specs=[
            pl.BlockSpec((tm, d), lambda i: (i, 0)),
            pl.BlockSpec((None, None, d, 2 * d_ff), lambda i: (layer, which, 0, 0), **const),
            pl.BlockSpec((None, None, d_ff, d), lambda i: (layer, which, 0, 0), **const),
            pl.BlockSpec((None, None, 1, d), lambda i: (layer, ln_idx, 0, 0), **const),
            pl.BlockSpec((None, None, 1, d), lambda i: (layer, ln_idx, 0, 0), **const),
        ],
        out_specs=pl.BlockSpec((tm, d), lambda i: (i, 0)),
        out_shape=jax.ShapeDtypeStruct((n, d), F32),
        scratch_shapes=[pltpu.VMEM((tm, d_ff), BF16)],
        compiler_params=pltpu.CompilerParams(
            dimension_semantics=("arbitrary",), vmem_limit_bytes=VMEM_LIMIT_BYTES),
        name=f"ffn_ln_{layer}_{which}",
    )(x2d, w1, w2, ln_g, ln_b)


def _causal_conv(val, tail_ref, w_ref, k):
    sub = V7X_SUBLANES
    nb = k - 1
    ts, c = val.shape
    tail = val[ts - nb * sub:, :].reshape(nb, sub, c)
    g = lax.broadcasted_iota(jnp.int32, (nb, sub, c), 1)
    wrapped = jnp.where(g == 0, pltpu.roll(tail_ref[...], 1, axis=1), pltpu.roll(tail, 1, axis=1))
    tail_ref[...] = tail
    ext = jnp.concatenate([wrapped.reshape(nb * sub, c), val], axis=0)
    out = val * w_ref[k - 1:k, :]
    for i in range(1, k):
        out = out + ext[(nb - i) * sub:(nb - i) * sub + ts, :] * w_ref[k - 1 - i:k - i, :]
    return out


def _chunk_scan(a, b, carry_ref):
    sub = V7X_SUBLANES
    ts, c = a.shape
    steps = ts // sub
    blk = lambda v, r: v[r * sub:(r + 1) * sub, :]
    p, q = blk(a, 0), blk(b, 0)
    for r in range(1, steps):
        q = blk(a, r) * q + blk(b, r)
        p = blk(a, r) * p
    g = lax.broadcasted_iota(jnp.int32, (sub, c), 0)
    d = 1
    while d < sub:
        keep = g >= d
        q = p * jnp.where(keep, pltpu.roll(q, d, axis=0), 0.0) + q
        p = p * jnp.where(keep, pltpu.roll(p, d, axis=0), 1.0)
        d *= 2
    h0 = carry_ref[...]
    ends = p * h0 + q
    carry_ref[...] = ends[sub - 1:sub, :]
    h = jnp.where(g == 0, h0, pltpu.roll(ends, 1, axis=0))
    hs = []
    for r in range(steps):
        h = blk(a, r) * h + blk(b, r)
        hs.append(h)
    return jnp.concatenate(hs, axis=0)


def _mixer_kernel(x_ref, win_ref, bm_ref, scw_ref, rcw_ref, rcb_ref, rgw_ref, rgb_ref, ap_ref,
                  woc_ref, wor_ref, wo_ref, g_ref, b_ref, o_ref,
                  cv_tail_ref, xr_tail_ref, carry_ref,
                  *, ts, dc, dr, dm, heads, alpha):
    sub = V7X_SUBLANES
    steps = ts // sub
    s = pl.program_id(1)

    @pl.when(s == 0)
    def _():
        cv_tail_ref[...] = jnp.zeros_like(cv_tail_ref)
        xr_tail_ref[...] = jnp.zeros_like(xr_tail_ref)
        carry_ref[...] = jnp.zeros_like(carry_ref)

    x = x_ref[...]
    xb = x.astype(BF16)

    c_b, c_c, c_v = slice(0, dc), slice(dc, 2 * dc), slice(2 * dc, 3 * dc)
    c_xr, c_yr = slice(3 * dc, 3 * dc + dr), slice(3 * dc + dr, 3 * dc + 2 * dr)
    c_ga = slice(3 * dc + 2 * dr, 3 * dc + 2 * dr + dm)
    c_gb = slice(3 * dc + 2 * dr + dm, 3 * dc + 2 * dr + 2 * dm)

    p_xr = _dot(xb, win_ref[:, c_xr])
    p_c = _dot(xb, win_ref[:, c_c])
    p_v = _dot(xb, win_ref[:, c_v])
    xc = _causal_conv(p_xr, xr_tail_ref, rcw_ref, RNN_CONV_K) + rcb_ref[...]
    xcb = xc.astype(BF16)
    hd = dr // heads
    gi = jnp.concatenate(
        [_dot(xcb[:, h * hd:(h + 1) * hd], rgw_ref[0, h]) for h in range(heads)], axis=-1)
    gr = jnp.concatenate(
        [_dot(xcb[:, h * hd:(h + 1) * hd], rgw_ref[1, h]) for h in range(heads)], axis=-1)
    conv_a = _causal_conv(p_c * p_v, cv_tail_ref, scw_ref, SHORT_CONV_K)
    p_b = _dot(xb, win_ref[:, c_b])
    p_yr = _dot(xb, win_ref[:, c_yr])
    p_ga = _dot(xb, win_ref[:, c_ga])

    gate_in = _sigmoid(gi + rgb_ref[0:1, :])
    gate_rec = _sigmoid(gr + rgb_ref[1:2, :])
    nap = -ap_ref[...]
    softplus = jnp.maximum(nap, 0.0) + jnp.log1p(jnp.exp(-jnp.abs(nap)))
    log_a = (-LRU_C) * gate_rec * softplus
    a = jnp.exp(log_a)
    mult = jnp.sqrt(-jnp.tanh(log_a) * (a * a + 1.0))
    row = lax.broadcasted_iota(jnp.int32, (ts, 1), 0)
    mult = jnp.where(jnp.logical_and(row == 0, s == 0), 1.0, mult)
    u_a = (p_b * conv_a).astype(BF16)
    h = _chunk_scan(a, xc * gate_in * mult, carry_ref)

    y_a = _dot(u_a, woc_ref[...])
    p_gb = _dot(xb, win_ref[:, c_gb])
    y_b = _dot((h * _gelu_tanh(p_yr)).astype(BF16), wor_ref[...])
    mix = (_sigmoid(p_ga + bm_ref[:, 0:dm]) * y_a
           + _sigmoid(p_gb + bm_ref[:, dm:2 * dm]) * y_b).astype(BF16)

    blk = ts // OUT_SPLIT
    for lo in range(0, ts, blk):
        o_ref[lo:lo + blk, :] = _layer_norm(
            alpha * x[lo:lo + blk, :] + _dot(mix[lo:lo + blk, :], wo_ref[...]), g_ref[...], b_ref[...])


def _mixer_ln(x2d, w_in, b_merge, sc_w, rc_w, rc_b, rg_w, rg_b, a_param,
              w_out_conv, w_out_rnn, w_o, ln_g, ln_b, layer, *, alpha, ts, tiles_per_seq):
    n, dm = x2d.shape
    dc = w_out_conv.shape[1]
    dr = w_out_rnn.shape[1]
    heads = rg_w.shape[2]
    hd = rg_w.shape[3]
    d_in = w_in.shape[2]
    sub = V7X_SUBLANES
    assert n % (ts * tiles_per_seq) == 0 and ts % sub == 0
    const = dict(pipeline_mode=pl.Buffered(1))
    lsel3 = lambda b, s: (layer, 0, 0)
    tile = lambda b, s: (b * tiles_per_seq + s, 0)
    kern = functools.partial(_mixer_kernel, ts=ts, dc=dc, dr=dr, dm=dm, heads=heads, alpha=alpha)
    return pl.pallas_call(
        kern,
        grid=(n // (ts * tiles_per_seq), tiles_per_seq),
        in_specs=[
            pl.BlockSpec((ts, dm), tile),
            pl.BlockSpec((None, dm, d_in), lsel3, **const),
            pl.BlockSpec((None, 1, 2 * dm), lsel3, **const),
            pl.BlockSpec((None, SHORT_CONV_K, dc), lsel3, **const),
            pl.BlockSpec((None, RNN_CONV_K, dr), lsel3, **const),
            pl.BlockSpec((None, 1, dr), lsel3, **const),
            pl.BlockSpec((None, 2, heads, hd, hd), lambda b, s: (layer, 0, 0, 0, 0), **const),
            pl.BlockSpec((None, 2, dr), lsel3, **const),
            pl.BlockSpec((None, 1, dr), lsel3, **const),
            pl.BlockSpec((None, dc, dm), lsel3, **const),
            pl.BlockSpec((None, dr, dm), lsel3, **const),
            pl.BlockSpec((None, dm, dm), lsel3, **const),
            pl.BlockSpec((None, None, 1, dm), lambda b, s: (layer, 1, 0, 0), **const),
            pl.BlockSpec((None, None, 1, dm), lambda b, s: (layer, 1, 0, 0), **const),
        ],
        out_specs=pl.BlockSpec((ts, dm), tile),
        out_shape=jax.ShapeDtypeStruct((n, dm), F32),
        scratch_shapes=[
            pltpu.VMEM((SHORT_CONV_K - 1, sub, dc), F32),
            pltpu.VMEM((RNN_CONV_K - 1, sub, dr), F32),
            pltpu.VMEM((1, dr), F32),
        ],
        compiler_params=pltpu.CompilerParams(
            dimension_semantics=("arbitrary", "arbitrary"), vmem_limit_bytes=VMEM_LIMIT_BYTES),
        name=f"mixer_ln_{layer}",
    )(x2d, w_in, b_merge, sc_w, rc_w, rc_b, rg_w, rg_b, a_param,
      w_out_conv, w_out_rnn, w_o, ln_g, ln_b)


def _pick_tile(n, target):
    t = min(n, target)
    while n % t:
        t -= V7X_SUBLANES
    return t


def _to_chunked_rows(x, ts):
    bsz, seq, dm = x.shape
    sub = V7X_SUBLANES
    xt = x.reshape(bsz, seq // ts, sub, ts // sub, dm)
    return jnp.swapaxes(xt, 2, 3).reshape(bsz * seq, dm)


def _from_chunked_rows(x2d, bsz, seq, ts):
    sub = V7X_SUBLANES
    xt = x2d.reshape(bsz, seq // ts, ts // sub, sub, x2d.shape[-1])
    return jnp.swapaxes(xt, 2, 3).reshape(bsz, seq, x2d.shape[-1])


def kernel(x, w_in, b_merge, sc_w, rc_w, rc_b, rg_w, rg_b, a_param, w_out_conv, w_out_rnn, w_o,
           ffn_w1, ffn_w2, ln_g, ln_b):
    bsz, seq, dm = x.shape
    depth = w_in.shape[0]
    alpha = (2.0 * depth) ** 0.25
    ts = _pick_tile(seq, 512)
    tm = ts

    w_in_b = w_in.astype(BF16)
    rg_w_b = rg_w.astype(BF16)
    woc_b = w_out_conv.astype(BF16)
    wor_b = w_out_rnn.astype(BF16)
    wo_b = w_o.astype(BF16)
    w1_b = ffn_w1.astype(BF16)
    w2_b = ffn_w2.astype(BF16)
    bm3 = b_merge[:, None, :]
    rcb3 = rc_b[:, None, :]
    ap3 = a_param[:, None, :]
    g4 = ln_g[:, :, None, :]
    b4 = ln_b[:, :, None, :]

    x = _to_chunked_rows(x, ts)
    for l in range(depth):
        x = _ffn_ln(x, w1_b, w2_b, g4, b4, l, 0, 0, alpha=alpha, tm=tm)
        x = _mixer_ln(x, w_in_b, bm3, sc_w, rc_w, rcb3, rg_w_b, rg_b, ap3, woc_b, wor_b, wo_b, g4, b4, l,
                      alpha=alpha, ts=ts, tiles_per_seq=seq // ts)
        x = _ffn_ln(x, w1_b, w2_b, g4, b4, l, 1, 2, alpha=alpha, tm=tm)
    return _from_chunked_rows(x, bsz, seq, ts)
```

```python
import functools
import math
from typing import NamedTuple

import jax
import jax.numpy as jnp
from jax import lax
from jax.experimental import pallas as pl
from jax.experimental.pallas import tpu as pltpu

LRU_C = 8.0
LN_EPS = 1e-5
SHORT_CONV_K = 3
RNN_CONV_K = 4
OUT_SPLIT = 2
TOKEN_TILE = 512

V7X_SUBLANES = 8
V7X_BF16_ROWS = 16
V7X_MXU_COLS = 256
V7X_VMEM_BYTES = 64 * 1024 * 1024
VMEM_LIMIT_BYTES = V7X_VMEM_BYTES - 8 * 1024 * 1024

F32 = jnp.float32
BF16 = jnp.bfloat16
U32 = jnp.uint32


def _sigmoid(x):
    return 0.5 * jnp.tanh(0.5 * x) + 0.5


def _gelu_tanh(x):
    c = math.sqrt(2.0 / math.pi)
    return 0.5 * x * (1.0 + jnp.tanh(c * (x + 0.044715 * (x * x * x))))


def _layer_norm(z, g, b):
    mu = jnp.mean(z, axis=-1, keepdims=True)
    zc = z - mu
    var = jnp.mean(zc * zc, axis=-1, keepdims=True)
    return zc * lax.rsqrt(var + LN_EPS) * g + b


_dot = functools.partial(jnp.dot, preferred_element_type=F32)


def _unpack(w_u32):
    return pltpu.bitcast(w_u32, BF16)


class NextWeight(NamedTuple):
    src: jax.Array
    lead: tuple


def _chunk_rows(k, steps):
    r = V7X_BF16_ROWS
    while k % r or k // r > steps:
        r += V7X_BF16_ROWS
    return r


def _next_specs(nexts, steps, step_of):
    in_specs, out_specs, out_shapes = [], [], []
    for nw in nexts:
        k, n = nw.src.shape[-2:]
        r = _chunk_rows(k, steps)
        chunk = lambda *g, c=k // r: step_of(*g) * c // steps
        in_specs.append(pl.BlockSpec(
            (None,) * len(nw.lead) + (r, n), lambda *g, lead=nw.lead, chunk=chunk: lead + (chunk(*g), 0)))
        out_specs.append(pl.BlockSpec((r // 2, n), lambda *g, chunk=chunk: (chunk(*g), 0)))
        out_shapes.append(jax.ShapeDtypeStruct((k // 2, n), U32))
    return in_specs, out_specs, out_shapes


def _cast_next(src_refs, dst_refs):
    for src, dst in zip(src_refs, dst_refs):
        dst[...] = pltpu.bitcast(src[...].astype(BF16), U32)


def _cast_kernel(*refs):
    n = len(refs) // 2
    _cast_next(refs[:n], refs[n:])


def _cast_weights(nexts, steps):
    in_specs, out_specs, out_shapes = _next_specs(nexts, steps, lambda i: i)
    return pl.pallas_call(
        _cast_kernel, grid=(steps,), in_specs=in_specs, out_specs=out_specs, out_shape=out_shapes,
        compiler_params=pltpu.CompilerParams(dimension_semantics=("arbitrary",)),
        name="cast_weights",
    )(*[nw.src for nw in nexts])


def _ffn_kernel(x_ref, w1_ref, w2_ref, g_ref, b_ref, *rest, n_next, d_ff, fc, alpha):
    next_src, o_ref, next_dst, act_ref = rest[:n_next], rest[n_next], rest[n_next + 1:-1], rest[-1]
    x = x_ref[...]
    xb = x.astype(BF16)
    for c in range(d_ff // fc):
        gate = _dot(xb, _unpack(w1_ref[:, c * fc:(c + 1) * fc]))
        up = _dot(xb, _unpack(w1_ref[:, d_ff + c * fc:d_ff + (c + 1) * fc]))
        act_ref[:, c * fc:(c + 1) * fc] = (gate * _sigmoid(gate) * up).astype(BF16)
    _cast_next(next_src, next_dst)
    blk = x.shape[0] // OUT_SPLIT
    for lo in range(0, x.shape[0], blk):
        y = _dot(act_ref[lo:lo + blk, :], _unpack(w2_ref[...]))
        o_ref[lo:lo + blk, :] = _layer_norm(alpha * x[lo:lo + blk, :] + 0.5 * y, g_ref[...], b_ref[...])


def _ffn_ln(x2d, w1p, w2p, ln_g, ln_b, layer, ln_idx, nexts, *, alpha, tm, name):
    n, d = x2d.shape
    d_ff = w2p.shape[0] * 2
    fc = V7X_MXU_COLS
    assert n % tm == 0 and d_ff % fc == 0
    steps = n // tm
    const = dict(pipeline_mode=pl.Buffered(1))
    nx_in, nx_out, nx_shapes = _next_specs(nexts, steps, lambda i: i)
    return pl.pallas_call(
        functools.partial(_ffn_kernel, n_next=len(nexts), d_ff=d_ff, fc=fc, alpha=alpha),
        grid=(steps,),
        in_specs=[
            pl.BlockSpec((tm, d), lambda i: (i, 0)),
            pl.BlockSpec(w1p.shape, lambda i: (0, 0), **const),
            pl.BlockSpec(w2p.shape, lambda i: (0, 0), **const),
            pl.BlockSpec((None, None, 1, d), lambda i: (layer, ln_idx, 0, 0), **const),
            pl.BlockSpec((None, None, 1, d), lambda i: (layer, ln_idx, 0, 0), **const),
        ] + nx_in,
        out_specs=[pl.BlockSpec((tm, d), lambda i: (i, 0))] + nx_out,
        out_shape=[jax.ShapeDtypeStruct((n, d), F32)] + nx_shapes,
        scratch_shapes=[pltpu.VMEM((tm, d_ff), BF16)],
        compiler_params=pltpu.CompilerParams(
            dimension_semantics=("arbitrary",), vmem_limit_bytes=VMEM_LIMIT_BYTES),
        name=name,
    )(x2d, w1p, w2p, ln_g, ln_b, *[nw.src for nw in nexts])


def _causal_conv(val, tail_ref, w_ref, k):
    sub = V7X_SUBLANES
    nb = k - 1
    ts, c = val.shape
    tail = val[ts - nb * sub:, :].reshape(nb, sub, c)
    g = lax.broadcasted_iota(jnp.int32, (nb, sub, c), 1)
    wrapped = jnp.where(g == 0, pltpu.roll(tail_ref[...], 1, axis=1), pltpu.roll(tail, 1, axis=1))
    tail_ref[...] = tail
    ext = jnp.concatenate([wrapped.reshape(nb * sub, c), val], axis=0)
    out = val * w_ref[k - 1:k, :]
    for i in range(1, k):
        out = out + ext[(nb - i) * sub:(nb - i) * sub + ts, :] * w_ref[k - 1 - i:k - i, :]
    return out


def _chunk_scan(a, b, carry_ref):
    sub = V7X_SUBLANES
    ts, c = a.shape
    steps = ts // sub
    blk = lambda v, r: v[r * sub:(r + 1) * sub, :]
    p, q = blk(a, 0), blk(b, 0)
    for r in range(1, steps):
        q = blk(a, r) * q + blk(b, r)
        p = blk(a, r) * p
    g = lax.broadcasted_iota(jnp.int32, (sub, c), 0)
    d = 1
    while d < sub:
        keep = g >= d
        q = p * jnp.where(keep, pltpu.roll(q, d, axis=0), 0.0) + q
        p = p * jnp.where(keep, pltpu.roll(p, d, axis=0), 1.0)
        d *= 2
    h0 = carry_ref[...]
    ends = p * h0 + q
    carry_ref[...] = ends[sub - 1:sub, :]
    h = jnp.where(g == 0, h0, pltpu.roll(ends, 1, axis=0))
    hs = []
    for r in range(steps):
        h = blk(a, r) * h + blk(b, r)
        hs.append(h)
    return jnp.concatenate(hs, axis=0)


def _mixer_kernel(x_ref, win_ref, bm_ref, scw_ref, rcw_ref, rcb_ref, rgw_ref, rgb_ref, ap_ref,
                  woc_ref, wor_ref, wo_ref, g_ref, b_ref, *rest,
                  n_next, ts, dc, dr, dm, heads, alpha):
    next_src, o_ref, next_dst = rest[:n_next], rest[n_next], rest[n_next + 1:2 * n_next + 1]
    cv_tail_ref, xr_tail_ref, carry_ref = rest[2 * n_next + 1:]
    s = pl.program_id(1)

    @pl.when(s == 0)
    def _():
        cv_tail_ref[...] = jnp.zeros_like(cv_tail_ref)
        xr_tail_ref[...] = jnp.zeros_like(xr_tail_ref)
        carry_ref[...] = jnp.zeros_like(carry_ref)

    x = x_ref[...]
    xb = x.astype(BF16)

    c_b, c_c, c_v = slice(0, dc), slice(dc, 2 * dc), slice(2 * dc, 3 * dc)
    c_xr, c_yr = slice(3 * dc, 3 * dc + dr), slice(3 * dc + dr, 3 * dc + 2 * dr)
    c_ga = slice(3 * dc + 2 * dr, 3 * dc + 2 * dr + dm)
    c_gb = slice(3 * dc + 2 * dr + dm, 3 * dc + 2 * dr + 2 * dm)

    p_xr = _dot(xb, _unpack(win_ref[:, c_xr]))
    p_c = _dot(xb, _unpack(win_ref[:, c_c]))
    p_v = _dot(xb, _unpack(win_ref[:, c_v]))
    xc = _causal_conv(p_xr, xr_tail_ref, rcw_ref, RNN_CONV_K) + rcb_ref[...]
    xcb = xc.astype(BF16)
    hd = dr // heads
    hp = hd // 2

    def gate_dot(gate):
        return jnp.concatenate(
            [_dot(xcb[:, h * hd:(h + 1) * hd],
                  _unpack(rgw_ref[(gate * heads + h) * hp:(gate * heads + h + 1) * hp, :]))
             for h in range(heads)], axis=-1)

    gi = gate_dot(0)
    gr = gate_dot(1)
    conv_a = _causal_conv(p_c * p_v, cv_tail_ref, scw_ref, SHORT_CONV_K)
    p_b = _dot(xb, _unpack(win_ref[:, c_b]))
    p_yr = _dot(xb, _unpack(win_ref[:, c_yr]))
    p_ga = _dot(xb, _unpack(win_ref[:, c_ga]))

    gate_in = _sigmoid(gi + rgb_ref[0:1, :])
    gate_rec = _sigmoid(gr + rgb_ref[1:2, :])
    nap = -ap_ref[...]
    softplus = jnp.maximum(nap, 0.0) + jnp.log1p(jnp.exp(-jnp.abs(nap)))
    log_a = (-LRU_C) * gate_rec * softplus
    a = jnp.exp(log_a)
    mult = jnp.sqrt(-jnp.tanh(log_a) * (a * a + 1.0))
    row = lax.broadcasted_iota(jnp.int32, (ts, 1), 0)
    mult = jnp.where(jnp.logical_and(row == 0, s == 0), 1.0, mult)
    u_a = (p_b * conv_a).astype(BF16)
    h = _chunk_scan(a, xc * gate_in * mult, carry_ref)
    _cast_next(next_src, next_dst)

    y_a = _dot(u_a, _unpack(woc_ref[...]))
    p_gb = _dot(xb, _unpack(win_ref[:, c_gb]))
    y_b = _dot((h * _gelu_tanh(p_yr)).astype(BF16), _unpack(wor_ref[...]))
    mix = (_sigmoid(p_ga + bm_ref[:, 0:dm]) * y_a
           + _sigmoid(p_gb + bm_ref[:, dm:2 * dm]) * y_b).astype(BF16)

    blk = ts // OUT_SPLIT
    for lo in range(0, ts, blk):
        o_ref[lo:lo + blk, :] = _layer_norm(
            alpha * x[lo:lo + blk, :] + _dot(mix[lo:lo + blk, :], _unpack(wo_ref[...])),
            g_ref[...], b_ref[...])


def _mixer_ln(x2d, winp, b_merge, sc_w, rc_w, rc_b, rgwp, rg_b, a_param, wocp, worp, wop,
              ln_g, ln_b, layer, nexts, *, heads, alpha, ts, tiles_per_seq):
    n, dm = x2d.shape
    dc = wocp.shape[0] * 2
    dr = worp.shape[0] * 2
    sub = V7X_SUBLANES
    assert n % (ts * tiles_per_seq) == 0 and ts % sub == 0
    steps = n // ts
    const = dict(pipeline_mode=pl.Buffered(1))
    whole = lambda b, s: (0, 0)
    lsel3 = lambda b, s: (layer, 0, 0)
    step_of = lambda b, s: b * tiles_per_seq + s
    tile = lambda b, s: (step_of(b, s), 0)
    nx_in, nx_out, nx_shapes = _next_specs(nexts, steps, step_of)
    kern = functools.partial(_mixer_kernel, n_next=len(nexts), ts=ts, dc=dc, dr=dr, dm=dm, heads=heads,
                             alpha=alpha)
    return pl.pallas_call(
        kern,
        grid=(steps // tiles_per_seq, tiles_per_seq),
        in_specs=[
            pl.BlockSpec((ts, dm), tile),
            pl.BlockSpec(winp.shape, whole, **const),
            pl.BlockSpec((None, 1, 2 * dm), lsel3, **const),
            pl.BlockSpec((None, SHORT_CONV_K, dc), lsel3, **const),
            pl.BlockSpec((None, RNN_CONV_K, dr), lsel3, **const),
            pl.BlockSpec((None, 1, dr), lsel3, **const),
            pl.BlockSpec(rgwp.shape, whole, **const),
            pl.BlockSpec((None, 2, dr), lsel3, **const),
            pl.BlockSpec((None, 1, dr), lsel3, **const),
            pl.BlockSpec(wocp.shape, whole, **const),
            pl.BlockSpec(worp.shape, whole, **const),
            pl.BlockSpec(wop.shape, whole, **const),
            pl.BlockSpec((None, None, 1, dm), lambda b, s: (layer, 1, 0, 0), **const),
            pl.BlockSpec((None, None, 1, dm), lambda b, s: (layer, 1, 0, 0), **const),
        ] + nx_in,
        out_specs=[pl.BlockSpec((ts, dm), tile)] + nx_out,
        out_shape=[jax.ShapeDtypeStruct((n, dm), F32)] + nx_shapes,
        scratch_shapes=[
            pltpu.VMEM((SHORT_CONV_K - 1, sub, dc), F32),
            pltpu.VMEM((RNN_CONV_K - 1, sub, dr), F32),
            pltpu.VMEM((1, dr), F32),
        ],
        compiler_params=pltpu.CompilerParams(
            dimension_semantics=("arbitrary", "arbitrary"), vmem_limit_bytes=VMEM_LIMIT_BYTES),
        name=f"mixer_ln_{layer}",
    )(x2d, winp, b_merge, sc_w, rc_w, rc_b, rgwp, rg_b, a_param, wocp, worp, wop, ln_g, ln_b,
      *[nw.src for nw in nexts])


def _pick_tile(n, target):
    t = min(n, target)
    while n % t:
        t -= V7X_SUBLANES
    return t


def _to_chunked_rows(x, ts):
    bsz, seq, dm = x.shape
    sub = V7X_SUBLANES
    xt = x.reshape(bsz, seq // ts, sub, ts // sub, dm)
    return jnp.swapaxes(xt, 2, 3).reshape(bsz * seq, dm)


def _from_chunked_rows(x2d, bsz, seq, ts):
    sub = V7X_SUBLANES
    xt = x2d.reshape(bsz, seq // ts, ts // sub, sub, x2d.shape[-1])
    return jnp.swapaxes(xt, 2, 3).reshape(bsz, seq, x2d.shape[-1])


def kernel(x, w_in, b_merge, sc_w, rc_w, rc_b, rg_w, rg_b, a_param, w_out_conv, w_out_rnn, w_o,
           ffn_w1, ffn_w2, ln_g, ln_b):
    bsz, seq, dm = x.shape
    depth = w_in.shape[0]
    heads, hd = rg_w.shape[2], rg_w.shape[3]
    alpha = (2.0 * depth) ** 0.25
    ts = _pick_tile(seq, TOKEN_TILE)
    steps = bsz * seq // ts

    rg_w2 = rg_w.reshape(depth, 2 * heads * hd, hd)
    bm3 = b_merge[:, None, :]
    rcb3 = rc_b[:, None, :]
    ap3 = a_param[:, None, :]
    g4 = ln_g[:, :, None, :]
    b4 = ln_b[:, :, None, :]

    def ffn_weights(l, j):
        return [NextWeight(ffn_w1, (l, j)), NextWeight(ffn_w2, (l, j))]

    def mixer_weights(l):
        return [NextWeight(w, (l,)) for w in (w_in, rg_w2, w_out_conv, w_out_rnn, w_o)]

    x = _to_chunked_rows(x, ts)
    wf = _cast_weights(ffn_weights(0, 0), steps)
    for l in range(depth):
        x, *wm = _ffn_ln(x, *wf, g4, b4, l, 0, mixer_weights(l), alpha=alpha, tm=ts, name=f"ffn_ln_{l}_0")
        x, *wf = _mixer_ln(x, wm[0], bm3, sc_w, rc_w, rcb3, wm[1], rg_b, ap3, wm[2], wm[3], wm[4], g4, b4, l,
                           ffn_weights(l, 1), heads=heads, alpha=alpha, ts=ts, tiles_per_seq=seq // ts)
        nexts = ffn_weights(l + 1, 0) if l + 1 < depth else []
        x, *wf = _ffn_ln(x, *wf, g4, b4, l, 2, nexts, alpha=alpha, tm=ts, name=f"ffn_ln_{l}_1")
    return _from_chunked_rows(x, bsz, seq, ts)
```
